```python
import math
import jax
import jax.numpy as jnp
from jax import lax
import numpy as np

D_MODEL = 1024
BATCH = 8
SEQ = 4096
DEPTH = 2
DEC_BATCH = 128
DEC_SEQ = 4
PAST_LEN = 16384
PAGE_SIZE = 128

HEAD_DIM = 64
BLK = 128
WIN_STEPS = 128
SWA_HEADS = 16
SWA_KV_HEADS = 2
SWA_WINDOW = WIN_STEPS * 1
SSM_EXPAND = 2
D_INNER = SSM_EXPAND * D_MODEL
SSM_HEAD_DIM = 64
SSM_HEADS = D_INNER // SSM_HEAD_DIM
SSM_GROUPS = 4
D_STATE = 128
CONV_W = 4
CONV_DIM = D_INNER + 2 * SSM_GROUPS * D_STATE
SSD_CHUNK = 128
DIL_PATTERNS = ((128, 1), (512, 4), (2048, 16))
N_DIL = 3
DIL_HEADS = 16
DIL_KV_HEADS = 16
D_FF = 4 * D_MODEL
NORM_EPS = 1e-5
N_EVEN = (DEPTH + 1) // 2
N_ODD = DEPTH // 2

SWA_Q = SWA_HEADS * HEAD_DIM
SWA_KV = SWA_KV_HEADS * HEAD_DIM
EVEN_IN = SWA_Q + 2 * SWA_KV + D_INNER + CONV_DIM + SSM_HEADS
EVEN_MIX = SWA_Q + D_INNER
DIL_Q = DIL_HEADS * HEAD_DIM
DIL_KV = DIL_KV_HEADS * HEAD_DIM
ODD_IN = N_DIL * (DIL_Q + 2 * DIL_KV)

kernel_name = 'hybrid_swa_ssd_dilated_decode_step'


def rmsnorm(x, w):
    xf = x.astype(jnp.float32)
    y = xf * lax.rsqrt(jnp.mean(xf * xf, axis=-1, keepdims=True) + NORM_EPS)
    return (y * w.astype(jnp.float32)).astype(x.dtype)


def alibi_slopes(n):
    return 2.0 ** (-8.0 * jnp.arange(1, n + 1, dtype=jnp.float32) / n)


def softmax_lse(s, sink=None):
    m = jnp.max(s, axis=-1)
    if sink is not None:
        m = jnp.maximum(m, sink)
    e = jnp.exp(s - m[..., None])
    den = jnp.sum(e, axis=-1)
    if sink is not None:
        den = den + jnp.exp(sink - m)
    return e / den[..., None], m + jnp.log(den)


def window_attn_prompt(q, k, v, dil, slopes, sink):
    bsz, s_len, n_kv, n_g, dh = q.shape
    unit = dil * BLK
    s_pad = -(-s_len // unit) * unit
    pad = s_pad - s_len
    n_steps = s_pad // dil
    nb = n_steps // BLK

    def strided(a):
        a = jnp.pad(a, [(0, 0), (0, pad)] + [(0, 0)] * (a.ndim - 2))
        a = a.reshape((bsz, n_steps, dil) + a.shape[2:])
        return jnp.swapaxes(a, 1, 2).reshape((bsz * dil, n_steps) + a.shape[3:])

    def with_prev(a):
        a = a.reshape(bsz * dil, nb, BLK, n_kv, dh)
        prev = jnp.pad(a[:, :-1], [(0, 0), (1, 0), (0, 0), (0, 0), (0, 0)])
        return jnp.concatenate([prev, a], axis=2)

    qb = strided(q).reshape(bsz * dil, nb, BLK, n_kv, n_g, dh)
    kb = with_prev(strided(k))
    vb = with_prev(strided(v))
    scores = jnp.einsum('bnqkgd,bnskd->bnqkgs', qb, kb, preferred_element_type=jnp.float32) * dh ** -0.5
    kpos = jnp.arange(2 * BLK)
    dist = (jnp.arange(BLK)[:, None] + BLK) - kpos[None, :]
    valid = (dist >= 0) & (dist <= WIN_STEPS)
    valid = valid[None] & ((jnp.arange(nb) > 0)[:, None, None] | (kpos >= BLK)[None, None, :])
    bias = -(slopes.reshape(n_kv, n_g)[None, :, :, None]
             * (dist * dil).astype(jnp.float32)[:, None, None, :])
    scores = jnp.where(valid[None, :, :, None, None, :], scores + bias, -jnp.inf)
    p, lse = softmax_lse(scores, sink)
    o = jnp.einsum('bnqkgs,bnskd->bnqkgd', p.astype(v.dtype), vb)

    def unstrided(a):
        a = a.reshape((bsz, dil, n_steps) + a.shape[3:])
        return jnp.swapaxes(a, 1, 2).reshape((bsz, s_pad) + a.shape[3:])[:, :s_len]

    return unstrided(o), unstrided(lse)


def window_attn_sample(q, k_cat, v_cat, n_past, dil, slopes, sink):
    t_len, n_kv, n_g, dh = q.shape[1:]
    steps = jnp.arange(WIN_STEPS + 1)
    idx = n_past + jnp.arange(t_len)[:, None] - dil * steps[None, :]
    valid = idx >= 0
    idx = jnp.maximum(idx, 0)
    kg = k_cat[:, idx]
    vg = v_cat[:, idx]
    scores = jnp.einsum('btkgd,btjkd->btkgj', q, kg, preferred_element_type=jnp.float32) * dh ** -0.5
    bias = -(slopes.reshape(n_kv, n_g)[:, :, None] * (steps * dil).astype(jnp.float32))
    scores = jnp.where(valid[None, :, None, None, :], scores + bias, -jnp.inf)
    p, lse = softmax_lse(scores, sink)
    o = jnp.einsum('btkgj,btjkd->btkgd', p.astype(v_cat.dtype), vg)
    return o, lse


def ssd_scan(x, dt, a, bm, cm, h0):
    bsz, t_len, n_h, p_dim = x.shape
    n_grp, n_st = bm.shape[2:]
    hg = n_h // n_grp
    q_len = min(SSD_CHUNK, t_len)
    t_pad = -(-t_len // q_len) * q_len
    nc = t_pad // q_len
    pad = [(0, 0), (0, t_pad - t_len)]
    x = jnp.pad(x.astype(jnp.float32), pad + [(0, 0), (0, 0)])
    dt = jnp.pad(dt, pad + [(0, 0)])
    bm = jnp.pad(bm.astype(jnp.float32), pad + [(0, 0), (0, 0)]).reshape(bsz, nc, q_len, n_grp, n_st)
    cm = jnp.pad(cm.astype(jnp.float32), pad + [(0, 0), (0, 0)]).reshape(bsz, nc, q_len, n_grp, n_st)
    xdt = (x * dt[..., None]).reshape(bsz, nc, q_len, n_grp, hg, p_dim)
    acum = jnp.cumsum((dt * a).reshape(bsz, nc, q_len, n_grp, hg), axis=2)
    at = jnp.moveaxis(acum, 2, -1)
    causal = jnp.tril(jnp.ones((q_len, q_len), dtype=bool))
    decay = jnp.exp(jnp.where(causal, at[..., :, None] - at[..., None, :], -jnp.inf))
    cb = jnp.einsum('bclgn,bcsgn->bcgls', cm, bm)
    y = jnp.einsum('bcghls,bcsghp->bclghp', cb[:, :, :, None] * decay, xdt)
    to_end = jnp.exp(acum[:, :, -1:] - acum)
    st = jnp.einsum('bclgn,bclghp->bcghpn', bm, xdt * to_end[..., None])

    def step(h, inp):
        da, s_c = inp
        return h * da[..., None, None] + s_c, h

    h_last, h_in = lax.scan(step, h0.astype(jnp.float32).reshape(bsz, n_grp, hg, p_dim, n_st),
                            (jnp.moveaxis(jnp.exp(acum[:, :, -1]), 1, 0), jnp.moveaxis(st, 1, 0)))
    y = y + jnp.einsum('bclgn,cbghpn->bclghp', cm, h_in) * jnp.exp(acum)[..., None]
    y = y.reshape(bsz, t_pad, n_h, p_dim)[:, :t_len]
    return y, h_last.reshape(bsz, n_h, p_dim, n_st)


def ssm_mixer(z, xbc, dt_raw, conv_w, conv_b, dt_bias, a_log, d_skip, norm_w, conv_state, h0):
    bsz, t_len = xbc.shape[:2]
    xpad = jnp.concatenate([conv_state.astype(xbc.dtype), xbc], axis=1)
    conv = conv_b
    for j in range(CONV_W):
        conv = conv + xpad[:, j:j + t_len] * conv_w[j]
    xbc_c = jax.nn.silu(conv)
    nb_ = SSM_GROUPS * D_STATE
    xs = xbc_c[..., :D_INNER].reshape(bsz, t_len, SSM_HEADS, SSM_HEAD_DIM)
    bm = xbc_c[..., D_INNER:D_INNER + nb_].reshape(bsz, t_len, SSM_GROUPS, D_STATE)
    cm = xbc_c[..., D_INNER + nb_:].reshape(bsz, t_len, SSM_GROUPS, D_STATE)
    dt = jax.nn.softplus(dt_raw.astype(jnp.float32) + dt_bias.astype(jnp.float32))
    a = -jnp.exp(a_log.astype(jnp.float32))
    y, h_last = ssd_scan(xs, dt, a, bm, cm, h0)
    y = y + d_skip.astype(jnp.float32)[:, None] * xs.astype(jnp.float32)
    g = y.reshape(bsz, t_len, D_INNER) * jax.nn.silu(z.astype(jnp.float32))
    g = g.reshape(bsz, t_len, SSM_GROUPS, D_INNER // SSM_GROUPS)
    g = g * lax.rsqrt(jnp.mean(g * g, axis=-1, keepdims=True) + NORM_EPS)
    out = (g.reshape(bsz, t_len, D_INNER) * norm_w.astype(jnp.float32)).astype(xbc.dtype)
    return out, xpad[:, -(CONV_W - 1):], h_last


def even_mixer(h, w_in, sinks, conv_w, conv_b, dt_bias, a_log, d_skip, ssm_norm, w_out, past):
    bsz, t_len = h.shape[:2]
    proj = h @ w_in
    o1 = SWA_Q
    o2 = o1 + SWA_KV
    o3 = o2 + SWA_KV
    o4 = o3 + D_INNER
    o5 = o4 + CONV_DIM
    q = proj[..., :o1].reshape(bsz, t_len, SWA_KV_HEADS, SWA_HEADS // SWA_KV_HEADS, HEAD_DIM)
    k = proj[..., o1:o2].reshape(bsz, t_len, SWA_KV_HEADS, HEAD_DIM)
    v = proj[..., o2:o3].reshape(bsz, t_len, SWA_KV_HEADS, HEAD_DIM)
    z = proj[..., o3:o4]
    xbc = proj[..., o4:o5]
    dt_raw = proj[..., o5:]
    slopes = alibi_slopes(SWA_HEADS)
    sink = sinks.astype(jnp.float32).reshape(SWA_KV_HEADS, SWA_HEADS // SWA_KV_HEADS)
    if past is None:
        o_a, _ = window_attn_prompt(q, k, v, 1, slopes, sink)
        n_keep = min(SWA_WINDOW, t_len)
        new_k, new_v = k[:, -n_keep:], v[:, -n_keep:]
        conv_state = jnp.zeros((bsz, CONV_W - 1, CONV_DIM), h.dtype)
        h0 = jnp.zeros((bsz, SSM_HEADS, SSM_HEAD_DIM, D_STATE), jnp.float32)
    else:
        past_k, past_v, conv_state, h0 = past
        n_past = past_k.shape[1]
        k_cat = jnp.concatenate([past_k.astype(k.dtype), k], axis=1)
        v_cat = jnp.concatenate([past_v.astype(v.dtype), v], axis=1)
        o_a, _ = window_attn_sample(q, k_cat, v_cat, n_past, 1, slopes, sink)
        new_k, new_v = k_cat[:, -n_past:], v_cat[:, -n_past:]
    y_ssm, new_conv, new_h = ssm_mixer(z, xbc, dt_raw, conv_w, conv_b, dt_bias, a_log, d_skip,
                                       ssm_norm, conv_state, h0)
    mixed = jnp.concatenate([o_a.reshape(bsz, t_len, SWA_Q).astype(h.dtype), y_ssm], axis=-1)
    return mixed @ w_out, (new_k, new_v, new_conv, new_h)


def odd_mixer(h, w_in, w_out, past):
    bsz, t_len = h.shape[:2]
    proj = h @ w_in
    grp = DIL_HEADS // DIL_KV_HEADS
    q_all = proj[..., :N_DIL * DIL_Q].reshape(bsz, t_len, N_DIL, DIL_KV_HEADS, grp, HEAD_DIM)
    k_all = proj[..., N_DIL * DIL_Q:N_DIL * (DIL_Q + DIL_KV)].reshape(bsz, t_len, N_DIL, DIL_KV_HEADS, HEAD_DIM)
    v_all = proj[..., N_DIL * (DIL_Q + DIL_KV):].reshape(bsz, t_len, N_DIL, DIL_KV_HEADS, HEAD_DIM)
    slopes = alibi_slopes(DIL_HEADS)
    outs, lses, new_state = [], [], []
    for gi, (window, dil) in enumerate(DIL_PATTERNS):
        q, k, v = q_all[:, :, gi], k_all[:, :, gi], v_all[:, :, gi]
        if past is None:
            o, lse = window_attn_prompt(q, k, v, dil, slopes, None)
            n_keep = min(window, t_len)
            new_state += [k[:, -n_keep:], v[:, -n_keep:]]
        else:
            past_k, past_v = past[2 * gi], past[2 * gi + 1]
            n_past = past_k.shape[1]
            k_cat = jnp.concatenate([past_k.astype(k.dtype), k], axis=1)
            v_cat = jnp.concatenate([past_v.astype(v.dtype), v], axis=1)
            o, lse = window_attn_sample(q, k_cat, v_cat, n_past, dil, slopes, None)
            new_state += [k_cat[:, -n_past:], v_cat[:, -n_past:]]
        outs.append(o)
        lses.append(lse)
    wts = jax.nn.softmax(jnp.stack(lses), axis=0)
    o = jnp.sum(wts[..., None] * jnp.stack(outs).astype(jnp.float32), axis=0)
    return o.reshape(bsz, t_len, DIL_Q).astype(h.dtype) @ w_out, tuple(new_state)


def trunk(x, past_even, past_odd, params):
    (norm_mix, norm_mlp, w_up, w_down, final_norm, even_w_in, even_sinks, even_conv_w, even_conv_b,
     even_dt_bias, even_a_log, even_d_skip, even_ssm_norm, even_w_out, odd_w_in, odd_w_out) = params
    new_even, new_odd = [], []
    for layer in range(DEPTH):
        li = layer // 2
        hn = rmsnorm(x, norm_mix[layer])
        if layer % 2 == 0:
            past = None if past_even is None else tuple(s[li] for s in past_even)
            mix, st = even_mixer(hn, even_w_in[li], even_sinks[li], even_conv_w[li], even_conv_b[li],
                                 even_dt_bias[li], even_a_log[li], even_d_skip[li], even_ssm_norm[li],
                                 even_w_out[li], past)
            new_even.append(st)
        else:
            past = None if past_odd is None else tuple(s[li] for s in past_odd)
            mix, st = odd_mixer(hn, odd_w_in[li], odd_w_out[li], past)
            new_odd.append(st)
        x = x + mix
        hn = rmsnorm(x, norm_mlp[layer])
        x = x + jnp.square(jax.nn.relu(hn @ w_up[layer])) @ w_down[layer]
    y = rmsnorm(x, final_norm)
    even_states = tuple(jnp.stack([st[i] for st in new_even]) for i in range(4))
    odd_states = tuple(jnp.stack([st[i] for st in new_odd]) for i in range(2 * N_DIL))
    return y, even_states, odd_states


def setup_inputs(seed: int = 0) -> dict:
    key = jax.random.key(seed)
    ks = jax.random.split(key, 32)

    def nrm(k, shape, scale=1.0):
        return scale * jax.random.normal(k, shape, jnp.float32)

    n_swa = min(SWA_WINDOW, PAST_LEN)
    d0, d1, d2 = [min(w, PAST_LEN) for w, _ in DIL_PATTERNS]
    dt_init = jnp.exp(jax.random.uniform(ks[20], (N_EVEN, SSM_HEADS), jnp.float32,
                                         math.log(1e-3), math.log(1e-1)))
    dt_bias = dt_init + jnp.log(-jnp.expm1(-dt_init))
    a_log = jnp.log(jax.random.uniform(ks[21], (N_EVEN, SSM_HEADS), jnp.float32, 1.0, 16.0))
    return {
        'x_prompt': nrm(ks[0], (BATCH, SEQ, D_MODEL)),
        'x_sample': nrm(ks[1], (DEC_BATCH, DEC_SEQ, D_MODEL)),
        'cache_swa_k': nrm(ks[2], (N_EVEN, DEC_BATCH, n_swa, SWA_KV_HEADS, HEAD_DIM)),
        'cache_swa_v': nrm(ks[3], (N_EVEN, DEC_BATCH, n_swa, SWA_KV_HEADS, HEAD_DIM)),
        'state_conv': nrm(ks[4], (N_EVEN, DEC_BATCH, CONV_W - 1, CONV_DIM)),
        'state_ssm': nrm(ks[5], (N_EVEN, DEC_BATCH, SSM_HEADS, SSM_HEAD_DIM, D_STATE), 0.1),
        'cache_dil0_k': nrm(ks[6], (N_ODD, DEC_BATCH, d0, DIL_KV_HEADS, HEAD_DIM)),
        'cache_dil0_v': nrm(ks[7], (N_ODD, DEC_BATCH, d0, DIL_KV_HEADS, HEAD_DIM)),
        'cache_dil1_k': nrm(ks[8], (N_ODD, DEC_BATCH, d1, DIL_KV_HEADS, HEAD_DIM)),
        'cache_dil1_v': nrm(ks[9], (N_ODD, DEC_BATCH, d1, DIL_KV_HEADS, HEAD_DIM)),
        'cache_dil2_k': nrm(ks[10], (N_ODD, DEC_BATCH, d2, DIL_KV_HEADS, HEAD_DIM)),
        'cache_dil2_v': nrm(ks[11], (N_ODD, DEC_BATCH, d2, DIL_KV_HEADS, HEAD_DIM)),
        'norm_mix': 1.0 + nrm(ks[12], (DEPTH, D_MODEL), 0.02),
        'norm_mlp': 1.0 + nrm(ks[13], (DEPTH, D_MODEL), 0.02),
        'w_up': nrm(ks[14], (DEPTH, D_MODEL, D_FF), D_MODEL ** -0.5),
        'w_down': nrm(ks[15], (DEPTH, D_FF, D_MODEL), D_FF ** -0.5),
        'final_norm': 1.0 + nrm(ks[16], (D_MODEL,), 0.02),
        'even_w_in': nrm(ks[17], (N_EVEN, D_MODEL, EVEN_IN), D_MODEL ** -0.5),
        'even_sinks': nrm(ks[18], (N_EVEN, SWA_HEADS)),
        'even_conv_w': nrm(ks[19], (N_EVEN, CONV_W, CONV_DIM), CONV_W ** -0.5),
        'even_conv_b': nrm(ks[22], (N_EVEN, CONV_DIM), 0.02),
        'even_dt_bias': dt_bias,
        'even_a_log': a_log,
        'even_d_skip': 1.0 + nrm(ks[23], (N_EVEN, SSM_HEADS), 0.02),
        'even_ssm_norm': 1.0 + nrm(ks[24], (N_EVEN, D_INNER), 0.02),
        'even_w_out': nrm(ks[25], (N_EVEN, EVEN_MIX, D_MODEL), EVEN_MIX ** -0.5),
        'odd_w_in': nrm(ks[26], (N_ODD, D_MODEL, ODD_IN), D_MODEL ** -0.5),
        'odd_w_out': nrm(ks[27], (N_ODD, DIL_Q, D_MODEL), DIL_Q ** -0.5),
    }


def reference(x_prompt, x_sample, cache_swa_k, cache_swa_v, state_conv, state_ssm,
              cache_dil0_k, cache_dil0_v, cache_dil1_k, cache_dil1_v, cache_dil2_k, cache_dil2_v,
              norm_mix, norm_mlp, w_up, w_down, final_norm,
              even_w_in, even_sinks, even_conv_w, even_conv_b, even_dt_bias, even_a_log, even_d_skip,
              even_ssm_norm, even_w_out, odd_w_in, odd_w_out):
    params = (norm_mix, norm_mlp, w_up, w_down, final_norm, even_w_in, even_sinks, even_conv_w, even_conv_b,
              even_dt_bias, even_a_log, even_d_skip, even_ssm_norm, even_w_out, odd_w_in, odd_w_out)
    y_prompt, ep, op = trunk(x_prompt, None, None, params)
    y_sample, es, osm = trunk(x_sample, (cache_swa_k, cache_swa_v, state_conv, state_ssm),
                              (cache_dil0_k, cache_dil0_v, cache_dil1_k, cache_dil1_v, cache_dil2_k, cache_dil2_v),
                              params)
    return (y_prompt, y_sample,
            ep[0], es[0], ep[1], es[1], ep[2], es[2], ep[3], es[3],
            op[0], osm[0], op[1], osm[1], op[2], osm[2], op[3], osm[3], op[4], osm[4], op[5], osm[5])
```

```python
import functools
import math

import jax
import jax.numpy as jnp
import numpy as np
from jax import lax
from jax.experimental import pallas as pl
from jax.experimental.pallas import tpu as pltpu

F32 = jnp.float32
BF16 = jnp.bfloat16

LANES = 128
HEAD_DIM = 64
BLK = 128
WIN_STEPS = 128
NORM_EPS = 1e-5
CONV_W = 4
SSD_CHUNK = 128
D_STATE = 128
SSM_GROUPS = 4
DIL_PATTERNS = ((128, 1), (512, 4), (2048, 16))
VMEM_LIMIT = 56 * 1024 * 1024


def _cparams(*sem):
    return pltpu.CompilerParams(dimension_semantics=sem, vmem_limit_bytes=VMEM_LIMIT)


def _rms(x, w):
    return x * lax.rsqrt(jnp.mean(x * x, axis=-1, keepdims=True) + NORM_EPS) * w


def _dot(a, b):
    return jnp.dot(a, b, preferred_element_type=F32)


def _dot_nt(a, b):
    return lax.dot_general(a, b, (((1,), (1,)), ((), ())), preferred_element_type=F32)


def _dot_tn(a, b):
    return lax.dot_general(a, b, (((0,), (0,)), ((), ())), preferred_element_type=F32)


def _norm_matmul_kernel(x_ref, g_ref, w_ref, o_ref, xn_ref):
    @pl.when(pl.program_id(1) == 0)
    def _():
        xn_ref[...] = _rms(x_ref[...], g_ref[...]).astype(BF16)

    o_ref[...] = _dot(xn_ref[...], w_ref[...])


def norm_matmul(x, g, w, *, tm, tn):
    m, d = x.shape
    n = w.shape[1]
    return pl.pallas_call(
        _norm_matmul_kernel,
        grid=(m // tm, n // tn),
        in_specs=[pl.BlockSpec((tm, d), lambda i, j: (i, 0)),
                  pl.BlockSpec((1, d), lambda i, j: (0, 0)),
                  pl.BlockSpec((d, tn), lambda i, j: (0, j))],
        out_specs=pl.BlockSpec((tm, tn), lambda i, j: (i, j)),
        out_shape=jax.ShapeDtypeStruct((m, n), F32),
        scratch_shapes=[pltpu.VMEM((tm, d), BF16)],
        compiler_params=_cparams("parallel", "arbitrary"),
        name="norm_matmul",
    )(x, g.reshape(1, d), w)


def _even_out_kernel(x_ref, a_ref, y_ref, wa_ref, wy_ref, o_ref):
    acc = _dot(a_ref[...].astype(BF16), wa_ref[...])
    acc = acc + _dot(y_ref[...].astype(BF16), wy_ref[...])
    o_ref[...] = x_ref[...] + acc


def even_out_proj(x, o_a, y_ssm, w_a, w_y, *, tm):
    m, d = x.shape
    ka, ky = o_a.shape[1], y_ssm.shape[1]
    return pl.pallas_call(
        _even_out_kernel,
        grid=(m // tm,),
        in_specs=[pl.BlockSpec((tm, d), lambda i: (i, 0)),
                  pl.BlockSpec((tm, ka), lambda i: (i, 0)),
                  pl.BlockSpec((tm, ky), lambda i: (i, 0)),
                  pl.BlockSpec((ka, d), lambda i: (0, 0)),
                  pl.BlockSpec((ky, d), lambda i: (0, 0))],
        out_specs=pl.BlockSpec((tm, d), lambda i: (i, 0)),
        out_shape=jax.ShapeDtypeStruct((m, d), F32),
        compiler_params=_cparams("parallel"),
        name="even_out_proj",
    )(x, o_a, y_ssm, w_a, w_y)


def _odd_out_kernel(x_ref, o0_ref, o1_ref, o2_ref, l0_ref, l1_ref, l2_ref, w_ref, o_ref):
    l0, l1, l2 = l0_ref[...], l1_ref[...], l2_ref[...]
    mx = jnp.maximum(jnp.maximum(l0, l1), l2)
    e0, e1, e2 = jnp.exp(l0 - mx), jnp.exp(l1 - mx), jnp.exp(l2 - mx)
    den = e0 + e1 + e2
    merged = (e0 / den) * o0_ref[...] + (e1 / den) * o1_ref[...] + (e2 / den) * o2_ref[...]
    o_ref[...] = x_ref[...] + _dot(merged.astype(BF16), w_ref[...])


def odd_out_proj(x, outs, lses, w, *, tm):
    m, d = x.shape
    row = pl.BlockSpec((tm, d), lambda i: (i, 0))
    return pl.pallas_call(
        _odd_out_kernel,
        grid=(m // tm,),
        in_specs=[row] * 7 + [pl.BlockSpec((d, d), lambda i: (0, 0))],
        out_specs=row,
        out_shape=jax.ShapeDtypeStruct((m, d), F32),
        compiler_params=_cparams("parallel"),
        name="odd_out_proj",
    )(x, *outs, *lses, w)


def _mlp_kernel(x_ref, g_ref, wu_ref, wd_ref, gf_ref, o_ref, xn_ref, acc_ref, *, final_norm):
    j = pl.program_id(1)

    @pl.when(j == 0)
    def _():
        xn_ref[...] = _rms(x_ref[...], g_ref[...]).astype(BF16)
        acc_ref[...] = jnp.zeros_like(acc_ref)

    h = jnp.maximum(_dot(xn_ref[...], wu_ref[...]), 0.0)
    acc_ref[...] += _dot((h * h).astype(BF16), wd_ref[...])

    @pl.when(j == pl.num_programs(1) - 1)
    def _():
        y = x_ref[...] + acc_ref[...]
        if final_norm:
            y = _rms(y, gf_ref[...])
        o_ref[...] = y


def mlp(x, g, w_up, w_down, g_final, *, tm, tf, final_norm):
    m, d = x.shape
    f = w_up.shape[1]
    return pl.pallas_call(
        functools.partial(_mlp_kernel, final_norm=final_norm),
        grid=(m // tm, f // tf),
        in_specs=[pl.BlockSpec((tm, d), lambda i, j: (i, 0)),
                  pl.BlockSpec((1, d), lambda i, j: (0, 0)),
                  pl.BlockSpec((d, tf), lambda i, j: (0, j)),
                  pl.BlockSpec((tf, d), lambda i, j: (j, 0)),
                  pl.BlockSpec((1, d), lambda i, j: (0, 0))],
        out_specs=pl.BlockSpec((tm, d), lambda i, j: (i, 0)),
        out_shape=jax.ShapeDtypeStruct((m, d), F32),
        scratch_shapes=[pltpu.VMEM((tm, d), BF16), pltpu.VMEM((tm, d), F32)],
        compiler_params=_cparams("parallel", "arbitrary"),
        name="mlp",
    )(x, g.reshape(1, d), w_up, w_down, g_final.reshape(1, d))


def _alibi_slope(h, n):
    return 2.0 ** (-8.0 * (h + 1) / n)


def _prompt_attn_kernel(*refs, n_pairs, q_per_kv, dil, has_sink, want_lse):
    it = iter(refs)
    q_ref, k_ref, v_ref, kp_ref, vp_ref, slope_ref = (next(it) for _ in range(6))
    sink_ref = next(it) if has_sink else None
    o_ref = next(it)
    lse_ref = next(it) if want_lse else None

    unit = pl.program_id(1)
    head0 = pl.program_id(2) * (2 * n_pairs)
    lo = lax.broadcasted_iota(jnp.int32, (BLK, LANES), 1) < HEAD_DIM
    lo2 = lax.broadcasted_iota(jnp.int32, (2 * BLK, LANES), 1) < HEAD_DIM
    row = lax.broadcasted_iota(jnp.int32, (BLK, 2 * BLK), 0)
    col = lax.broadcasted_iota(jnp.int32, (BLK, 2 * BLK), 1)
    dist = row + BLK - col
    valid = (dist >= 0) & (dist <= WIN_STEPS) & ((unit > 0) | (col >= BLK))
    distf = (dist * dil).astype(F32)

    def residue(r, carry):
        rows = pl.ds(0, BLK) if dil == 1 else pl.ds(r, BLK, stride=dil)
        if q_per_kv > 1:
            kk = jnp.concatenate([kp_ref[0, rows, :], k_ref[0, rows, :]], axis=0)
            vv = jnp.concatenate([vp_ref[0, rows, :], v_ref[0, rows, :]], axis=0)
            kr, vr = pltpu.roll(kk, HEAD_DIM, axis=1), pltpu.roll(vv, HEAD_DIM, axis=1)
            kdup = [jnp.where(lo2, kk, kr), jnp.where(lo2, kr, kk)]
            vdup = [jnp.where(lo2, vv, vr), jnp.where(lo2, vr, vv)]
        for j in range(n_pairs):
            sl = slice(LANES * j, LANES * (j + 1))
            qb = q_ref[0, rows, sl] * (HEAD_DIM ** -0.5)
            if q_per_kv > 1:
                kb, vb = kdup[(2 * j) // q_per_kv], vdup[(2 * j) // q_per_kv]
            else:
                kb = jnp.concatenate([kp_ref[0, rows, sl], k_ref[0, rows, sl]], axis=0)
                vb = jnp.concatenate([vp_ref[0, rows, sl], v_ref[0, rows, sl]], axis=0)
            qs = jnp.concatenate([jnp.where(lo, qb, 0.0), jnp.where(lo, 0.0, qb)], axis=0)
            s = _dot_nt(qs.astype(BF16), kb.astype(BF16))
            ps, rinv, lses = [], [], []
            for t in range(2):
                h = head0 + 2 * j + t
                sh = s[BLK * t:BLK * (t + 1)] - slope_ref[h] * distf
                sh = jnp.where(valid, sh, -jnp.inf)
                m = jnp.max(sh, axis=-1, keepdims=True)
                if has_sink:
                    m = jnp.maximum(m, sink_ref[h])
                e = jnp.exp(sh - m)
                den = jnp.sum(e, axis=-1, keepdims=True)
                if has_sink:
                    den = den + jnp.exp(sink_ref[h] - m)
                ps.append(e.astype(BF16))
                rinv.append(1.0 / den)
                lses.append(m + jnp.log(den))
            pcat = jnp.concatenate(ps, axis=1)
            vs = jnp.concatenate([jnp.where(lo2, vb, 0.0), jnp.where(lo2, 0.0, vb)], axis=0)
            o = _dot(pcat, vs.astype(BF16))
            o_ref[0, rows, sl] = o * jnp.where(lo, rinv[0], rinv[1])
            if want_lse:
                lse_ref[0, rows, sl] = jnp.where(lo, lses[0], lses[1])
        return carry

    if dil == 1:
        residue(0, 0)
    else:
        lax.fori_loop(0, dil, residue, 0)


def prompt_attention(proj, *, q_col, k_col, v_col, n_heads, q_per_kv, dil, pairs_per_step,
                     sinks=None, want_lse=False):
    bsz, s_len, c = proj.shape
    unit = dil * BLK
    assert s_len % unit == 0 and (n_heads // 2) % pairs_per_step == 0
    n_col = (n_heads // 2) // pairs_per_step
    qw = pairs_per_step * LANES
    if q_per_kv > 1:
        assert n_col == 1
        kw = (n_heads // q_per_kv) * HEAD_DIM
    else:
        kw = qw
    cq, ck = (n_heads * HEAD_DIM) // qw, ((n_heads // q_per_kv) * HEAD_DIM) // kw
    prev = lambda u: jnp.maximum(u - 1, 0)
    in_specs = [
        pl.BlockSpec((1, unit, qw), lambda b, u, j: (b, u, q_col * cq + j)),
        pl.BlockSpec((1, unit, kw), lambda b, u, j: (b, u, k_col * ck + j)),
        pl.BlockSpec((1, unit, kw), lambda b, u, j: (b, u, v_col * ck + j)),
        pl.BlockSpec((1, unit, kw), lambda b, u, j: (b, prev(u), k_col * ck + j)),
        pl.BlockSpec((1, unit, kw), lambda b, u, j: (b, prev(u), v_col * ck + j)),
        pl.BlockSpec(memory_space=pltpu.SMEM),
    ]
    slopes = jnp.asarray(np.array([_alibi_slope(h, n_heads) for h in range(n_heads)], np.float32))
    args = [proj, proj, proj, proj, proj, slopes]
    if sinks is not None:
        in_specs.append(pl.BlockSpec(memory_space=pltpu.SMEM))
        args.append(sinks.astype(F32))
    o_spec = pl.BlockSpec((1, unit, qw), lambda b, u, j: (b, u, j))
    o_shape = jax.ShapeDtypeStruct((bsz, s_len, n_heads * HEAD_DIM), F32)
    return pl.pallas_call(
        functools.partial(_prompt_attn_kernel, n_pairs=pairs_per_step, q_per_kv=q_per_kv, dil=dil,
                          has_sink=sinks is not None, want_lse=want_lse),
        grid=(bsz, s_len // unit, n_col),
        in_specs=in_specs,
        out_specs=[o_spec, o_spec] if want_lse else o_spec,
        out_shape=[o_shape, o_shape] if want_lse else o_shape,
        compiler_params=_cparams("parallel", "parallel", "parallel"),
        name="prompt_attention",
    )(*args)


def _silu(x):
    return x * (1.0 / (1.0 + jnp.exp(-x)))


def _softplus(x):
    return jnp.maximum(x, 0.0) + jnp.log1p(jnp.exp(-jnp.abs(x)))


def _expand_heads(v, e):
    hi = v.astype(BF16)
    lo = (v - hi.astype(F32)).astype(BF16)
    return _dot(hi, e) + _dot(lo, e)


def _ssd_prompt_kernel(z_ref, x_ref, b_ref, c_ref, dt_ref, cw_ref, cb_ref, dtb_ref, alog_ref,
                       dskip_ref, nw_ref, e_ref, y_ref, hout_ref, xpad_ref, h_ref,
                       *, n_groups, heads_per_group):
    chunk = pl.program_id(1)
    L = SSD_CHUNK
    gw = heads_per_group * HEAD_DIM
    di = n_groups * gw
    nb = n_groups * D_STATE

    @pl.when(chunk == 0)
    def _():
        xpad_ref[0:8, :] = jnp.zeros((8, di + 2 * nb), F32)
        h_ref[...] = jnp.zeros_like(h_ref)

    xpad_ref[8:8 + L, 0:di] = x_ref[0]
    xpad_ref[8:8 + L, di:di + nb] = b_ref[0]
    xpad_ref[8:8 + L, di + nb:] = c_ref[0]
    conv = cb_ref[...] + xpad_ref[pl.ds(5, L), :] * cw_ref[0:1, :]
    for j in range(1, CONV_W):
        conv = conv + xpad_ref[pl.ds(5 + j, L), :] * cw_ref[j:j + 1, :]
    xpad_ref[0:8, :] = xpad_ref[L:L + 8, :]
    xbc = _silu(conv)
    xs, bm, cm = xbc[:, :di], xbc[:, di:di + nb], xbc[:, di + nb:]

    dt = _softplus(dt_ref[0] + dtb_ref[...])
    d_a = dt * (-jnp.exp(alog_ref[...]))
    row = lax.broadcasted_iota(jnp.int32, (L, L), 0)
    col = lax.broadcasted_iota(jnp.int32, (L, L), 1)
    causal = row >= col
    acum = jnp.dot(causal.astype(F32), d_a, preferred_element_type=F32,
                   precision=lax.Precision.HIGHEST)
    acum_t = acum.T
    e_acum = jnp.exp(acum)
    to_end = jnp.exp(acum[L - 1:L, :] - acum)
    e = e_ref[...]
    dt_e = _expand_heads(dt, e)
    ea_e = _expand_heads(e_acum, e)
    w_e = _expand_heads(dt * to_end, e)
    xdt = xs * dt_e
    xw = xs * w_e

    lo = lax.broadcasted_iota(jnp.int32, (L, LANES), 1) < HEAD_DIM
    for g in range(n_groups):
        gs = slice(gw * g, gw * (g + 1))
        cmg = cm[:, D_STATE * g:D_STATE * (g + 1)].astype(BF16)
        bmg = bm[:, D_STATE * g:D_STATE * (g + 1)].astype(BF16)
        cb = _dot_nt(cmg, bmg)
        h_old = h_ref[:, gs]
        y_off = _dot(cmg, h_old.astype(BF16)) * ea_e[:, gs]
        pairs = []
        for jj in range(heads_per_group // 2):
            j = (heads_per_group // 2) * g + jj
            ms = []
            for t in range(2):
                h = 2 * j + t
                diff = acum[:, h:h + 1] - acum_t[h:h + 1, :]
                dec = jnp.exp(jnp.where(causal, diff, -jnp.inf))
                ms.append((cb * dec).astype(BF16))
            xp = xdt[:, LANES * j:LANES * (j + 1)]
            rhs = jnp.concatenate([jnp.where(lo, xp, 0.0), jnp.where(lo, 0.0, xp)],
                                  axis=0).astype(BF16)
            pairs.append(_dot(jnp.concatenate(ms, axis=1), rhs))
        st = _dot_tn(bmg, xw[:, gs].astype(BF16))
        h_ref[:, gs] = h_old * ea_e[L - 1:L, gs] + st
        y = jnp.concatenate(pairs, axis=1) + y_off + dskip_ref[:, gs] * xs[:, gs]
        gz = y * _silu(z_ref[0, :, gs])
        ms2 = jnp.mean(gz * gz, axis=-1, keepdims=True)
        y_ref[0, :, gs] = gz * lax.rsqrt(ms2 + NORM_EPS) * nw_ref[:, gs]

    @pl.when(chunk == pl.num_programs(1) - 1)
    def _():
        hout_ref[0] = h_ref[...]


def _head_expander(n_heads):
    r = lax.broadcasted_iota(jnp.int32, (LANES, n_heads * HEAD_DIM), 0)
    c = lax.broadcasted_iota(jnp.int32, (LANES, n_heads * HEAD_DIM), 1)
    return (c // HEAD_DIM == r).astype(BF16)


def _pad_lanes(v):
    return jnp.pad(v.astype(F32), (0, LANES - v.shape[0])).reshape(1, LANES)


def ssd_prompt(proj, cols, conv_w, conv_b, dt_bias, a_log, d_skip, norm_w, *, n_heads, n_groups):
    bsz, s_len, _ = proj.shape
    di = n_heads * HEAD_DIM
    nb = n_groups * D_STATE
    L = SSD_CHUNK
    assert s_len % L == 0
    cz, cx, cbm, ccm, cdt = cols
    full = lambda shape: pl.BlockSpec(shape, lambda b, c: (0,) * len(shape))
    return pl.pallas_call(
        functools.partial(_ssd_prompt_kernel, n_groups=n_groups, heads_per_group=n_heads // n_groups),
        grid=(bsz, s_len // L),
        in_specs=[pl.BlockSpec((1, L, di), lambda b, c: (b, c, cz)),
                  pl.BlockSpec((1, L, di), lambda b, c: (b, c, cx)),
                  pl.BlockSpec((1, L, nb), lambda b, c: (b, c, cbm)),
                  pl.BlockSpec((1, L, nb), lambda b, c: (b, c, ccm)),
                  pl.BlockSpec((1, L, LANES), lambda b, c: (b, c, cdt)),
                  full((CONV_W, di + 2 * nb)), full((1, di + 2 * nb)),
                  full((1, LANES)), full((1, LANES)), full((1, di)), full((1, di)),
                  full((LANES, di))],
        out_specs=[pl.BlockSpec((1, L, di), lambda b, c: (b, c, 0)),
                   pl.BlockSpec((1, D_STATE, di), lambda b, c: (b, 0, 0))],
        out_shape=[jax.ShapeDtypeStruct((bsz, s_len, di), F32),
                   jax.ShapeDtypeStruct((bsz, D_STATE, di), F32)],
        scratch_shapes=[pltpu.VMEM((L + 8, di + 2 * nb), F32), pltpu.VMEM((D_STATE, di), F32)],
        compiler_params=_cparams("parallel", "arbitrary"),
        name="ssd_prompt",
    )(proj, proj, proj, proj, proj, conv_w, conv_b.reshape(1, -1), _pad_lanes(dt_bias),
      _pad_lanes(a_log), jnp.repeat(d_skip.astype(F32), HEAD_DIM).reshape(1, di),
      norm_w.reshape(1, di), _head_expander(n_heads))


def _ssd_sample_kernel(z_ref, x_ref, b_ref, c_ref, dt_ref, cs_ref, h0_ref, cw_ref, cb_ref, dtb_ref,
                       alog_ref, dskip_ref, nw_ref, e_ref, y_ref, hout_ref, xpad_ref,
                       *, n_groups, heads_per_group):
    t_new = x_ref.shape[1]
    rows = 8
    gw = heads_per_group * HEAD_DIM
    di = n_groups * gw
    nb = n_groups * D_STATE

    xpad_ref[...] = jnp.zeros_like(xpad_ref)
    xpad_ref[8 - (CONV_W - 1):8, :] = cs_ref[0]
    xpad_ref[8:8 + t_new, 0:di] = x_ref[0]
    xpad_ref[8:8 + t_new, di:di + nb] = b_ref[0]
    xpad_ref[8:8 + t_new, di + nb:] = c_ref[0]
    conv = cb_ref[...] + xpad_ref[pl.ds(5, rows), :] * cw_ref[0:1, :]
    for j in range(1, CONV_W):
        conv = conv + xpad_ref[pl.ds(5 + j, rows), :] * cw_ref[j:j + 1, :]
    xbc = _silu(conv)
    xs, bm, cm = xbc[:, :di], xbc[:, di:di + nb], xbc[:, di + nb:]

    row = lax.broadcasted_iota(jnp.int32, (rows, LANES), 0)
    dt_in = jnp.concatenate([dt_ref[0], jnp.zeros((rows - t_new, LANES), F32)], axis=0)
    dt = jnp.where(row < t_new, _softplus(dt_in + dtb_ref[...]), 0.0)
    acum = dt * (-jnp.exp(alog_ref[...]))
    for sft in (1, 2, 4):
        acum = acum + jnp.where(row >= sft, pltpu.roll(acum, sft, axis=0), 0.0)
    e = e_ref[...]
    ea_e = _expand_heads(jnp.exp(acum), e)
    xdt = xs * _expand_heads(dt, e)
    xw = xs * _expand_heads(dt * jnp.exp(acum[rows - 1:rows, :] - acum), e)

    y = dskip_ref[...] * xs
    for d in range(t_new):
        bm_d = bm if d == 0 else pltpu.roll(bm, d, axis=0)
        xdt_d = xdt if d == 0 else pltpu.roll(xdt, d, axis=0)
        prod = cm * bm_d
        cb_e = jnp.concatenate(
            [jnp.broadcast_to(jnp.sum(prod[:, D_STATE * g:D_STATE * (g + 1)], axis=-1, keepdims=True),
                              (rows, gw)) for g in range(n_groups)], axis=1)
        term = cb_e * xdt_d
        if d > 0:
            dec = jnp.where(row >= d, jnp.exp(acum - pltpu.roll(acum, d, axis=0)), 0.0)
            term = term * _expand_heads(dec, e)
        y = y + term

    sub = lax.broadcasted_iota(jnp.int32, (rows, LANES), 0)
    ones2 = jnp.where(sub < 2, 1.0, 0.0).astype(BF16)
    for g in range(n_groups):
        gs = slice(gw * g, gw * (g + 1))
        cmg = cm[:, D_STATE * g:D_STATE * (g + 1)].astype(BF16)
        bmg = bm[:, D_STATE * g:D_STATE * (g + 1)].astype(BF16)
        h_old = h0_ref[0, gs, :]
        yg = y[:, gs] + _dot_nt(cmg, h_old.astype(BF16)) * ea_e[:, gs]
        gz = yg * _silu(jnp.concatenate([z_ref[0, :, gs], jnp.zeros((rows - t_new, gw), F32)], axis=0))
        ms2 = jnp.mean(gz * gz, axis=-1, keepdims=True)
        y_ref[0, :, gs] = (gz * lax.rsqrt(ms2 + NORM_EPS) * nw_ref[:, gs])[0:t_new]
        a_row = ea_e[rows - 1:rows, gs]
        a_hi = a_row.astype(BF16)
        a_lo = (a_row - a_hi.astype(F32)).astype(BF16)
        a8 = jnp.concatenate([a_hi, a_lo, jnp.zeros((rows - 2, gw), BF16)], axis=0)
        decay = _dot_tn(a8, ones2)
        st = _dot_tn(xw[:, gs].astype(BF16), bmg)
        hout_ref[0, gs, :] = h_old * decay + st


def ssd_sample(ps, cols, conv_state, h0, conv_w, conv_b, dt_bias, a_log, d_skip, norm_w,
               *, n_heads, n_groups):
    bsz, t_new, _ = ps.shape
    di = n_heads * HEAD_DIM
    nb = n_groups * D_STATE
    cz, cx, cbm, ccm, cdt = cols
    full = lambda shape: pl.BlockSpec(shape, lambda b: (0,) * len(shape))
    return pl.pallas_call(
        functools.partial(_ssd_sample_kernel, n_groups=n_groups, heads_per_group=n_heads // n_groups),
        grid=(bsz,),
        in_specs=[pl.BlockSpec((1, t_new, di), lambda b: (b, 0, cz)),
                  pl.BlockSpec((1, t_new, di), lambda b: (b, 0, cx)),
                  pl.BlockSpec((1, t_new, nb), lambda b: (b, 0, cbm)),
                  pl.BlockSpec((1, t_new, nb), lambda b: (b, 0, ccm)),
                  pl.BlockSpec((1, t_new, LANES), lambda b: (b, 0, cdt)),
                  pl.BlockSpec((1, CONV_W - 1, di + 2 * nb), lambda b: (b, 0, 0)),
                  pl.BlockSpec((1, di, D_STATE), lambda b: (b, 0, 0)),
                  full((CONV_W, di + 2 * nb)), full((1, di + 2 * nb)),
                  full((1, LANES)), full((1, LANES)), full((1, di)), full((1, di)),
                  full((LANES, di))],
        out_specs=[pl.BlockSpec((1, t_new, di), lambda b: (b, 0, 0)),
                   pl.BlockSpec((1, di, D_STATE), lambda b: (b, 0, 0))],
        out_shape=[jax.ShapeDtypeStruct((bsz, t_new, di), F32),
                   jax.ShapeDtypeStruct((bsz, di, D_STATE), F32)],
        scratch_shapes=[pltpu.VMEM((16, di + 2 * nb), F32)],
        compiler_params=_cparams("parallel"),
        name="ssd_sample",
    )(ps, ps, ps, ps, ps, conv_state, h0, conv_w, conv_b.reshape(1, -1), _pad_lanes(dt_bias),
      _pad_lanes(a_log), jnp.repeat(d_skip.astype(F32), HEAD_DIM).reshape(1, di),
      norm_w.reshape(1, di), _head_expander(n_heads))


def _sample_attn_kernel(*refs, hc, g_q, has_sink):
    it = iter(refs)
    (q_ref, kn_ref, vn_ref, kt_ref, vt_ref, dist_ref, neg_ref, distn_ref, negn_ref,
     slope_ref) = (next(it) for _ in range(10))
    sink_ref = next(it) if has_sink else None
    o_ref, lse_ref, kto_ref, vto_ref = (next(it) for _ in range(4))
    t_new = q_ref.shape[1]
    w = kt_ref.shape[3]
    rows = 8 * g_q

    q8 = jnp.concatenate([q_ref[0] * (HEAD_DIM ** -0.5),
                          jnp.zeros((8 - t_new, q_ref.shape[2]), F32)], axis=0)
    kn = jnp.concatenate([kn_ref[0], jnp.zeros((LANES - t_new, hc * HEAD_DIM), F32)], axis=0)
    vn = jnp.concatenate([vn_ref[0], jnp.zeros((LANES - t_new, hc * HEAD_DIM), F32)], axis=0)
    tile_q = lambda a: a if g_q == 1 else jnp.concatenate([a] * g_q, axis=0)
    dist, neg = tile_q(dist_ref[...]), tile_q(neg_ref[...])
    distn, negn = tile_q(distn_ref[...]), tile_q(negn_ref[...])
    lane = lax.broadcasted_iota(jnp.int32, (HEAD_DIM, LANES), 1)
    ins = [[pltpu.roll(src[:, LANES * p:LANES * (p + 1)].T, LANES - t_new, axis=1)
            for p in range(hc // 2)] for src in (kn, vn)]

    o_pieces, l_pieces = [], []
    for hh in range(hc):
        kt, vt = kt_ref[0, hh], vt_ref[0, hh]
        hs = slice(HEAD_DIM * hh, HEAD_DIM * (hh + 1))
        qs = jnp.concatenate([q8[:, HEAD_DIM * (hh * g_q + g):HEAD_DIM * (hh * g_q + g + 1)]
                              for g in range(g_q)], axis=0).astype(BF16)
        slope = slope_ref[0, rows * hh:rows * (hh + 1), 0:1]
        s = _dot(qs, kt.astype(BF16)) - slope * dist + neg
        sn = _dot_nt(qs, kn[:, hs].astype(BF16)) - slope * distn + negn
        m = jnp.maximum(jnp.max(s, axis=-1, keepdims=True), jnp.max(sn, axis=-1, keepdims=True))
        if has_sink:
            sink = sink_ref[0, rows * hh:rows * (hh + 1), 0:1]
            m = jnp.maximum(m, sink)
        e, en = jnp.exp(s - m), jnp.exp(sn - m)
        den = jnp.sum(e, axis=-1, keepdims=True) + jnp.sum(en, axis=-1, keepdims=True)
        if has_sink:
            den = den + jnp.exp(sink - m)
        o = _dot_nt(e.astype(BF16), vt.astype(BF16)) + _dot(en.astype(BF16), vn[:, hs].astype(BF16))
        o = o * (1.0 / den)
        lse = m + jnp.log(den)
        for g in range(g_q):
            o_pieces.append(o[8 * g:8 * (g + 1)])
            l_pieces.append(jnp.broadcast_to(lse[8 * g:8 * (g + 1)], (8, HEAD_DIM)))
        for src, which, dst in ((kt, 0, kto_ref), (vt, 1, vto_ref)):
            rolled = pltpu.roll(src, w - t_new, axis=1)
            new_t = ins[which][hh // 2][HEAD_DIM * (hh % 2):HEAD_DIM * (hh % 2 + 1)]
            if w > LANES:
                dst[0, hh, :, 0:w - LANES] = rolled[:, 0:w - LANES]
            dst[0, hh, :, w - LANES:] = jnp.where(lane >= LANES - t_new, new_t, rolled[:, w - LANES:])
    for p in range(len(o_pieces) // 2):
        ps_ = slice(LANES * p, LANES * (p + 1))
        o_ref[0, :, ps_] = jnp.concatenate([o_pieces[2 * p], o_pieces[2 * p + 1]], axis=1)
        lse_ref[0, :, ps_] = jnp.concatenate([l_pieces[2 * p], l_pieces[2 * p + 1]], axis=1)


def sample_attention(ps, cache_k, cache_v, *, q_off, k_off, v_off, n_heads, q_per_kv, dil, hc,
                     sinks=None):
    bsz, t_new, _ = ps.shape
    _, _, w, n_kv, _ = cache_k.shape
    assert w == WIN_STEPS * dil and n_kv * q_per_kv == n_heads and n_kv % hc == 0 and hc % 2 == 0
    kt = jnp.transpose(cache_k[0], (0, 2, 3, 1))
    vt = jnp.transpose(cache_v[0], (0, 2, 3, 1))
    n_hb = n_kv // hc
    qw, kw = hc * q_per_kv * HEAD_DIM, hc * HEAD_DIM
    assert q_off % qw == 0 and k_off % kw == 0 and v_off % kw == 0

    i = np.arange(8)[:, None]
    real = i < t_new
    d_past = w + i - np.arange(w)[None, :]
    ok = (d_past % dil == 0) & (d_past // dil <= WIN_STEPS)
    dist = np.where(real, d_past, 0).astype(np.float32)
    neg = np.where(~real | ok, 0.0, -np.inf).astype(np.float32)
    r = np.arange(LANES)[None, :]
    d_new = i - r
    okn = (d_new >= 0) & (d_new % dil == 0) & (d_new // dil <= WIN_STEPS) & (r < t_new)
    distn = np.where(real & okn, d_new, 0).astype(np.float32)
    negn = np.where(~real | okn, 0.0, -np.inf).astype(np.float32)
    slopes = np.array([_alibi_slope(h, n_heads) for h in range(n_heads)], np.float32)
    per_row = lambda v: jnp.broadcast_to(v.reshape(n_hb, hc * q_per_kv, 1, 1),
                                         (n_hb, hc * q_per_kv, 8, LANES)).reshape(n_hb, -1, LANES)
    rows = hc * q_per_kv * 8

    full = lambda a: pl.BlockSpec(a.shape, lambda b, h: (0,) * a.ndim)
    cache_spec = pl.BlockSpec((1, hc, HEAD_DIM, w), lambda b, h: (b, h, 0, 0))
    row_spec = pl.BlockSpec((1, rows, LANES), lambda b, h: (h, 0, 0))
    o_spec = pl.BlockSpec((1, 8, qw), lambda b, h: (b, 0, h))
    tables = [jnp.asarray(t) for t in (dist, neg, distn, negn)]
    in_specs = [pl.BlockSpec((1, t_new, qw), lambda b, h: (b, 0, q_off // qw + h)),
                pl.BlockSpec((1, t_new, kw), lambda b, h: (b, 0, k_off // kw + h)),
                pl.BlockSpec((1, t_new, kw), lambda b, h: (b, 0, v_off // kw + h)),
                cache_spec, cache_spec] + [full(t) for t in tables] + [row_spec]
    args = [ps, ps, ps, kt, vt] + tables + [per_row(jnp.asarray(slopes))]
    if sinks is not None:
        in_specs.append(row_spec)
        args.append(per_row(sinks.astype(F32)))
    o_shape = jax.ShapeDtypeStruct((bsz, 8, n_heads * HEAD_DIM), F32)
    c_shape = jax.ShapeDtypeStruct((bsz, n_kv, HEAD_DIM, w), F32)
    o, lse, kto, vto = pl.pallas_call(
        functools.partial(_sample_attn_kernel, hc=hc, g_q=q_per_kv, has_sink=sinks is not None),
        grid=(bsz, n_hb),
        in_specs=in_specs,
        out_specs=[o_spec, o_spec, cache_spec, cache_spec],
        out_shape=[o_shape, o_shape, c_shape, c_shape],
        compiler_params=_cparams("parallel", "parallel"),
        name="sample_attention",
    )(*args)
    back = lambda c: jnp.transpose(c, (0, 3, 1, 2))[None]
    return o[:, :t_new], lse[:, :t_new], back(kto), back(vto)


_EVEN_SEGMENTS = ("z", "x", "q", "B", "C", "k", "v", "dt")
EVEN_PROJ_TN = 512


def _even_layout(d_inner, n_bc, swa_q, swa_kv, n_ssm_heads):
    src, off = {}, 0
    for name, width in (("q", swa_q), ("k", swa_kv), ("v", swa_kv), ("z", d_inner), ("x", d_inner),
                        ("B", n_bc), ("C", n_bc), ("dt", n_ssm_heads)):
        src[name] = (off, width)
        off += width
    out, pos = {}, 0
    for name in _EVEN_SEGMENTS:
        s_off, width = src[name]
        slot = max(width, LANES)
        assert pos % slot == 0, (name, pos, slot)
        out[name] = (s_off, width, pos)
        pos += slot
    total = -(-pos // EVEN_PROJ_TN) * EVEN_PROJ_TN
    return out, total


def _trunk(x, past, weights, *, tm):
    (norm_mix, norm_mlp, w_up, w_down, final_norm, w_even, lay, sinks, conv_w, conv_b, dt_bias,
     a_log, d_skip, ssm_norm, w_out_a, w_out_y, w_odd, w_odd_out) = weights
    bsz, t_len, d = x.shape
    m = bsz * t_len
    n_ssm_heads = dt_bias.shape[0]
    d_inner = n_ssm_heads * HEAD_DIM
    n_bc = SSM_GROUPS * D_STATE
    swa_q = lay["q"][1]
    n_swa_heads = swa_q // HEAD_DIM
    swa_per_kv = swa_q // lay["k"][1]
    xf = x.reshape(m, d)

    proj = norm_matmul(xf, norm_mix[0], w_even, tm=tm, tn=EVEN_PROJ_TN).reshape(bsz, t_len, -1)
    col = lambda name, width: lay[name][2] // width
    ssd_cols = (col("z", d_inner), col("x", d_inner), col("B", n_bc), col("C", n_bc), col("dt", LANES))
    seg = lambda a, name: a[..., lay[name][2]:lay[name][2] + lay[name][1]]
    assert t_len >= CONV_W - 1
    tail = proj[:, -(CONV_W - 1):]
    new_conv = jnp.concatenate([seg(tail, "x"), seg(tail, "B"), seg(tail, "C")], axis=-1)[None]
    if past is None:
        o_a = prompt_attention(proj, q_col=col("q", swa_q), k_col=col("k", LANES), v_col=col("v", LANES),
                               n_heads=n_swa_heads, q_per_kv=swa_per_kv, dil=1,
                               pairs_per_step=n_swa_heads // 2, sinks=sinks)
        y_ssm, h_t = ssd_prompt(proj, ssd_cols, conv_w, conv_b, dt_bias, a_log, d_skip, ssm_norm,
                                n_heads=n_ssm_heads, n_groups=SSM_GROUPS)
        n_keep = min(WIN_STEPS, t_len)
        kv_shape = (1, bsz, n_keep, lay["k"][1] // HEAD_DIM, HEAD_DIM)
        new_k = seg(proj, "k")[:, -n_keep:].reshape(kv_shape)
        new_v = seg(proj, "v")[:, -n_keep:].reshape(kv_shape)
        new_h = jnp.swapaxes(h_t, 1, 2).reshape(1, bsz, n_ssm_heads, HEAD_DIM, D_STATE)
    else:
        swa_k, swa_v, conv_state, ssm_state = past[:4]
        o_a, _, new_k, new_v = sample_attention(
            proj, swa_k, swa_v, q_off=lay["q"][2], k_off=lay["k"][2], v_off=lay["v"][2],
            n_heads=n_swa_heads, q_per_kv=swa_per_kv, dil=1, hc=lay["k"][1] // HEAD_DIM, sinks=sinks)
        y_ssm, new_h = ssd_sample(proj, ssd_cols, conv_state[0], ssm_state[0].reshape(bsz, d_inner, D_STATE),
                                  conv_w, conv_b, dt_bias, a_log, d_skip, ssm_norm,
                                  n_heads=n_ssm_heads, n_groups=SSM_GROUPS)
        new_h = new_h.reshape(1, bsz, n_ssm_heads, HEAD_DIM, D_STATE)
    x1 = even_out_proj(xf, o_a.reshape(m, swa_q), y_ssm.reshape(m, d_inner), w_out_a, w_out_y,
                       tm=min(tm, 512))
    x2 = mlp(x1, norm_mlp[0], w_up[0], w_down[0], final_norm, tm=tm, tf=512, final_norm=False)

    n_dil = len(DIL_PATTERNS)
    dil_q = w_odd.shape[1] // (3 * n_dil)
    n_dil_heads = dil_q // HEAD_DIM
    proj2 = norm_matmul(x2, norm_mix[1], w_odd, tm=tm, tn=dil_q).reshape(bsz, t_len, -1)
    outs, lses, odd_state = [], [], []
    for gi, (window, dil) in enumerate(DIL_PATTERNS):
        if past is None:
            o, lse = prompt_attention(proj2, q_col=gi, k_col=n_dil + gi, v_col=2 * n_dil + gi,
                                      n_heads=n_dil_heads, q_per_kv=1, dil=dil,
                                      pairs_per_step=n_dil_heads // 2 if dil == 1 else 1,
                                      want_lse=True)
            n_keep = min(window, t_len)
            for base in (n_dil, 2 * n_dil):
                kv = proj2[:, -n_keep:, (base + gi) * dil_q:(base + gi + 1) * dil_q]
                odd_state.append(kv.reshape(1, bsz, n_keep, n_dil_heads, HEAD_DIM))
        else:
            o, lse, nk, nv = sample_attention(
                proj2, past[4 + 2 * gi], past[5 + 2 * gi], q_off=gi * dil_q, k_off=(n_dil + gi) * dil_q,
                v_off=(2 * n_dil + gi) * dil_q, n_heads=n_dil_heads, q_per_kv=1, dil=dil, hc=4)
            odd_state += [nk, nv]
        outs.append(o.reshape(m, dil_q))
        lses.append(lse.reshape(m, dil_q))
    x3 = odd_out_proj(x2, outs, lses, w_odd_out, tm=min(tm, 256))
    y = mlp(x3, norm_mlp[1], w_up[1], w_down[1], final_norm, tm=tm, tf=512, final_norm=True)
    return y.reshape(bsz, t_len, d), (new_k, new_v, new_conv, new_h), tuple(odd_state)


def kernel(x_prompt, x_sample, cache_swa_k, cache_swa_v, state_conv, state_ssm, cache_dil0_k, cache_dil0_v, cache_dil1_k, cache_dil1_v, cache_dil2_k, cache_dil2_v, norm_mix, norm_mlp, w_up, w_down, final_norm, even_w_in, even_sinks, even_conv_w, even_conv_b, even_dt_bias, even_a_log, even_d_skip, even_ssm_norm, even_w_out, odd_w_in, odd_w_out):
    assert norm_mix.shape[0] == 2 and even_w_in.shape[0] == 1 and odd_w_in.shape[0] == 1
    n_ssm_heads = even_dt_bias.shape[1]
    d_inner = n_ssm_heads * HEAD_DIM
    n_bc = SSM_GROUPS * D_STATE
    swa_kv = cache_swa_k.shape[3] * HEAD_DIM
    swa_q = even_w_out.shape[1] - d_inner
    lay, total = _even_layout(d_inner, n_bc, swa_q, swa_kv, n_ssm_heads)
    w_in = even_w_in[0]
    d = w_in.shape[0]
    pieces, pos = [], 0
    for name in _EVEN_SEGMENTS:
        s_off, width, o_off = lay[name]
        if o_off > pos:
            pieces.append(jnp.zeros((d, o_off - pos), w_in.dtype))
        pieces.append(w_in[:, s_off:s_off + width])
        pos = o_off + width
    pieces.append(jnp.zeros((d, total - pos), w_in.dtype))
    w_even = jnp.concatenate(pieces, axis=1).astype(BF16)
    weights = (norm_mix, norm_mlp, w_up.astype(BF16), w_down.astype(BF16), final_norm, w_even, lay,
               even_sinks[0], even_conv_w[0], even_conv_b[0], even_dt_bias[0], even_a_log[0],
               even_d_skip[0], even_ssm_norm[0], even_w_out[0, :swa_q].astype(BF16),
               even_w_out[0, swa_q:].astype(BF16), odd_w_in[0].astype(BF16), odd_w_out[0].astype(BF16))
    y_p, ep, op = _trunk(x_prompt, None, weights, tm=1024)
    past = (cache_swa_k, cache_swa_v, state_conv, state_ssm, cache_dil0_k, cache_dil0_v,
            cache_dil1_k, cache_dil1_v, cache_dil2_k, cache_dil2_v)
    y_s, es, os_ = _trunk(x_sample, past, weights, tm=x_sample.shape[0] * x_sample.shape[1])
    return (y_p, y_s, ep[0], es[0], ep[1], es[1], ep[2], es[2], ep[3], es[3],
            op[0], os_[0], op[1], os_[1], op[2], os_[2], op[3], os_[3], op[4], os_[4], op[5], os_[5])
```
